```python
import math
import jax, jax.numpy as jnp
from jax import lax
import numpy as np

D_MODEL = 1024
BATCH = 16
SEQ = 2048
DEPTH = 1
DEC_BATCH = 8
DEC_SEQ = 2048
PAST_LEN = 128

MIX_W = D_MODEL
POOL_W = MIX_W // 2
POOL_WINDOWS = (2, 4, 8, 16)
N_POOL_GROUPS = len(POOL_WINDOWS)
POOL_GC = POOL_W // N_POOL_GROUPS
N_HEADS = 8
QK_NOPE = 64
QK_ROPE = 32
QK_DIM = QK_NOPE + QK_ROPE
V_DIM = 64
ATT_W = N_HEADS * V_DIM
Q_LORA = 384
KV_LORA = 256
ROPE_BASE = 10000.0
Q_BLOCK = 128
ATTN_SCALE = 1.0 / math.sqrt(QK_DIM)
IN_W = POOL_W + Q_LORA + KV_LORA + QK_ROPE
D_FF = int(math.ceil(D_MODEL * 8 / 3 / 256) * 256)
PLE_DIM = 256
EPS = 1e-6

kernel_name = "hybrid_pool_mla_encoder"


def rms_norm(x, g):
    xf = x.astype(jnp.float32)
    y = xf * lax.rsqrt(jnp.mean(xf * xf, axis=-1, keepdims=True) + EPS)
    return (y * g.astype(jnp.float32)).astype(x.dtype)


def rope_tables(S, dtype):
    inv = ROPE_BASE ** (-jnp.arange(0, QK_ROPE, 2, dtype=jnp.float32) / QK_ROPE)
    ang = jnp.arange(S, dtype=jnp.float32)[:, None] * inv[None, :]
    return jnp.cos(ang).astype(dtype), jnp.sin(ang).astype(dtype)


def apply_rope(x, cos, sin):
    x1, x2 = jnp.split(x, 2, axis=-1)
    c = cos[None, :, None, :]
    s = sin[None, :, None, :]
    return jnp.concatenate([x1 * c - x2 * s, x1 * s + x2 * c], axis=-1)


def multiscale_pool(u, w_pool, pool_scale):
    B, S, _ = u.shape
    ug = u.reshape(B, S, N_POOL_GROUPS, POOL_GC)
    csum = jnp.cumsum(ug.astype(jnp.float32), axis=1)
    cs = jnp.concatenate([jnp.zeros_like(csum[:, :1]), csum], axis=1)
    t = jnp.arange(S)
    means = []
    for g, w in enumerate(POOL_WINDOWS):
        lo = jnp.clip(t - w // 2, 0, S)
        hi = jnp.clip(t - w // 2 + w, 0, S)
        csg = cs[:, :, g]
        cnt = (hi - lo).astype(jnp.float32)[None, :, None]
        means.append((csg[:, hi] - csg[:, lo]) / cnt)
    mean = jnp.stack(means, axis=2).astype(u.dtype)
    y = jnp.einsum('bsgc,gcd->bsgd', mean - ug, w_pool)
    return y.reshape(B, S, POOL_W) * pool_scale


def block_attention(q, k, v):
    B, S, H, D = q.shape
    nb = S // Q_BLOCK
    qb = q.reshape(B, nb, Q_BLOCK, H, D).transpose(1, 0, 2, 3, 4)

    def one(qblk):
        s = jnp.einsum('bqhd,bkhd->bhqk', qblk, k).astype(jnp.float32) * ATTN_SCALE
        p = jax.nn.softmax(s, axis=-1).astype(v.dtype)
        return jnp.einsum('bhqk,bkhd->bqhd', p, v)

    o = lax.map(one, qb)
    return o.transpose(1, 0, 2, 3, 4).reshape(B, S, H * V_DIM)


def encoder_layer(h, p, cos, sin, ln1, w_in, w_pool, pool_scale, q_a_norm, w_qb, kv_a_norm, w_kvb,
                  q_norm, k_norm, w_o, ln2, w_gate, w_up, w_down, ple_norm, w_ple_gate, w_ple_proj):
    B, S, _ = h.shape
    u = rms_norm(h, ln1)
    z = u @ w_in
    pool_in = z[..., :POOL_W]
    c_q = z[..., POOL_W:POOL_W + Q_LORA]
    c_kv = z[..., POOL_W + Q_LORA:POOL_W + Q_LORA + KV_LORA]
    k_r = z[..., POOL_W + Q_LORA + KV_LORA:]

    y_pool = multiscale_pool(pool_in, w_pool, pool_scale)

    q = (rms_norm(c_q, q_a_norm) @ w_qb).reshape(B, S, N_HEADS, QK_DIM)
    kv = (rms_norm(c_kv, kv_a_norm) @ w_kvb).reshape(B, S, N_HEADS, QK_NOPE + V_DIM)
    k_nope, v = kv[..., :QK_NOPE], kv[..., QK_NOPE:]
    k = jnp.concatenate([k_nope, jnp.broadcast_to(k_r[:, :, None, :], (B, S, N_HEADS, QK_ROPE))], axis=-1)
    q = rms_norm(q, q_norm)
    k = rms_norm(k, k_norm)
    q = jnp.concatenate([q[..., :QK_NOPE], apply_rope(q[..., QK_NOPE:], cos, sin)], axis=-1)
    k = jnp.concatenate([k[..., :QK_NOPE], apply_rope(k[..., QK_NOPE:], cos, sin)], axis=-1)
    y_att = block_attention(q, k, v)

    h = h + jnp.concatenate([y_pool, y_att], axis=-1) @ w_o

    u2 = rms_norm(h, ln2)
    h = h + (jax.nn.silu(u2 @ w_gate) * (u2 @ w_up)) @ w_down

    gate = jax.nn.sigmoid(rms_norm(h, ple_norm) @ w_ple_gate)
    return h + gate * (p @ w_ple_proj)


def setup_inputs(seed: int = 0) -> dict:
    key = jax.random.key(seed)
    ks = jax.random.split(key, 24)
    f = jnp.float32

    def nrm(k, shape, fan_in):
        return jax.random.normal(k, shape, f) * (fan_in ** -0.5)

    def gain(k, shape):
        return 1.0 + 0.1 * jax.random.normal(k, shape, f)

    L = DEPTH
    return {
        "x_prompt": jax.random.normal(ks[0], (BATCH, SEQ, D_MODEL), f),
        "x_sample": jax.random.normal(ks[1], (DEC_BATCH, DEC_SEQ, D_MODEL), f),
        "p_prompt": jax.random.normal(ks[2], (DEPTH, BATCH, SEQ, PLE_DIM), f),
        "p_sample": jax.random.normal(ks[3], (DEPTH, DEC_BATCH, DEC_SEQ, PLE_DIM), f),
        "ln1": gain(ks[4], (L, D_MODEL)),
        "w_in": nrm(ks[5], (L, D_MODEL, IN_W), D_MODEL),
        "w_pool": nrm(ks[6], (L, N_POOL_GROUPS, POOL_GC, POOL_GC), POOL_GC),
        "pool_scale": gain(ks[7], (L, POOL_W)),
        "q_a_norm": gain(ks[8], (L, Q_LORA)),
        "w_qb": nrm(ks[9], (L, Q_LORA, N_HEADS * QK_DIM), Q_LORA),
        "kv_a_norm": gain(ks[10], (L, KV_LORA)),
        "w_kvb": nrm(ks[11], (L, KV_LORA, N_HEADS * (QK_NOPE + V_DIM)), KV_LORA),
        "q_norm": gain(ks[12], (L, QK_DIM)),
        "k_norm": gain(ks[13], (L, QK_DIM)),
        "w_o": nrm(ks[14], (L, MIX_W, D_MODEL), MIX_W),
        "ln2": gain(ks[15], (L, D_MODEL)),
        "w_gate": nrm(ks[16], (L, D_MODEL, D_FF), D_MODEL),
        "w_up": nrm(ks[17], (L, D_MODEL, D_FF), D_MODEL),
        "w_down": nrm(ks[18], (L, D_FF, D_MODEL), D_FF),
        "ple_norm": gain(ks[19], (L, D_MODEL)),
        "w_ple_gate": nrm(ks[20], (L, D_MODEL, D_MODEL), D_MODEL),
        "w_ple_proj": nrm(ks[21], (L, PLE_DIM, D_MODEL), PLE_DIM),
    }


def reference(x_prompt, x_sample, p_prompt, p_sample, ln1, w_in, w_pool, pool_scale, q_a_norm, w_qb,
              kv_a_norm, w_kvb, q_norm, k_norm, w_o, ln2, w_gate, w_up, w_down, ple_norm, w_ple_gate,
              w_ple_proj):
    def run(x, p):
        cos, sin = rope_tables(x.shape[1], x.dtype)
        h = x
        for i in range(DEPTH):
            h = encoder_layer(h, p[i], cos, sin, ln1[i], w_in[i], w_pool[i], pool_scale[i], q_a_norm[i],
                              w_qb[i], kv_a_norm[i], w_kvb[i], q_norm[i], k_norm[i], w_o[i], ln2[i],
                              w_gate[i], w_up[i], w_down[i], ple_norm[i], w_ple_gate[i], w_ple_proj[i])
        return h

    y_prompt = run(x_prompt, p_prompt)
    y_sample = run(x_sample, p_sample)
    return (y_prompt, y_sample)
```

```python
import functools
import math

import jax
import jax.numpy as jnp
from jax import lax
from jax.experimental import pallas as pl
from jax.experimental.pallas import tpu as pltpu

D_MODEL = 1024
POOL_W = 512
POOL_WINDOWS = (2, 4, 8, 16)
POOL_GC = 128
N_HEADS = 8
QK_NOPE = 64
QK_ROPE = 32
QK_DIM = QK_NOPE + QK_ROPE
V_DIM = 64
Q_LORA = 384
KV_LORA = 256
ROPE_BASE = 10000.0
ATTN_SCALE = 1.0 / math.sqrt(QK_DIM)
D_FF = 2816
PLE_DIM = 256
EPS = 1e-6

LANES = 128
SUBLANES = 8
HEAD_SLOT = LANES
QK_SLOTS = N_HEADS * HEAD_SLOT
IN_W_PAD = POOL_W + Q_LORA + KV_LORA + HEAD_SLOT
POOL_HALO = SUBLANES
VMEM_LIMIT_BYTES = 60 * 1024 * 1024

F32 = jnp.float32
BF16 = jnp.bfloat16


def _dot(a, b):
    return jnp.dot(a, b, preferred_element_type=F32)


def _rms(x, g, n):
    ms = jnp.sum(x * x, axis=-1, keepdims=True) * (1.0 / n)
    return x * lax.rsqrt(ms + EPS) * g


def _proj_kernel(x_ref, ln1_ref, w_in_ref, qa_ref, w_qb_ref, kva_ref, w_kn_ref, w_v_ref, vone_ref,
                 qg_ref, kg_ref, c_ref, s1_ref, s2_ref, pool_ref, q_ref, k_ref, v_ref):
    u = _rms(x_ref[0], ln1_ref[...], D_MODEL).astype(BF16)
    z = _dot(u, w_in_ref[...])
    pool_ref[0] = z[:, :POOL_W]
    cq = _rms(z[:, POOL_W:POOL_W + Q_LORA], qa_ref[...], Q_LORA).astype(BF16)
    ckv = _rms(z[:, POOL_W + Q_LORA:POOL_W + Q_LORA + KV_LORA], kva_ref[...], KV_LORA).astype(BF16)
    k_rope = z[:, POOL_W + Q_LORA + KV_LORA:]
    qf = _dot(cq, w_qb_ref[...])
    kn = _dot(ckv, w_kn_ref[...])
    vv = _dot(ckv, w_v_ref[...])
    v_ref[0] = (vv + vone_ref[...]).astype(BF16)

    c, s1, s2 = c_ref[...], s1_ref[...], s2_ref[...]

    def norm_rope(xh, g, scale):
        ms = jnp.sum(xh * xh, axis=-1, keepdims=True) * (1.0 / QK_DIM)
        y = xh * (lax.rsqrt(ms + EPS) * scale) * g
        return y * c + pltpu.roll(y, HEAD_SLOT - 16, 1) * s1 + pltpu.roll(y, 16, 1) * s2

    for h in range(N_HEADS):
        sl = slice(h * HEAD_SLOT, (h + 1) * HEAD_SLOT)
        q_ref[0, :, sl] = norm_rope(qf[:, sl], qg_ref[...], ATTN_SCALE).astype(BF16)
        k_ref[0, :, sl] = norm_rope(kn[:, sl] + k_rope, kg_ref[...], 1.0).astype(BF16)


def _const_spec(shape):
    return pl.BlockSpec(shape, lambda *_: (0,) * len(shape), pipeline_mode=pl.Buffered(1))


def _projection(x, wts, tables, tm):
    B, S, _ = x.shape
    nj = S // tm
    row = lambda w: pl.BlockSpec((1, tm, w), lambda b, j: (b, j, 0))
    tab = pl.BlockSpec((tm, HEAD_SLOT), lambda b, j: (j, 0))
    consts = [wts[k] for k in ("ln1", "w_in", "q_a_norm", "w_qb", "kv_a_norm", "w_kn", "w_v", "v_one",
                               "q_norm", "k_norm")]
    return pl.pallas_call(
        _proj_kernel,
        grid=(B, nj),
        in_specs=[row(D_MODEL)] + [_const_spec(a.shape) for a in consts] + [tab, tab, tab],
        out_specs=[row(POOL_W), row(QK_SLOTS), row(QK_SLOTS), row(QK_SLOTS)],
        out_shape=[jax.ShapeDtypeStruct((B, S, POOL_W), F32)]
        + [jax.ShapeDtypeStruct((B, S, QK_SLOTS), BF16)] * 3,
        compiler_params=pltpu.CompilerParams(
            dimension_semantics=("arbitrary", "arbitrary"), vmem_limit_bytes=VMEM_LIMIT_BYTES),
        name="projection",
    )(x, *consts, *tables)


def _attn_kernel(q_ref, k_ref, v_ref, o_ref):
    tq = q_ref.shape[1]
    outs = []
    for hh in range(2):
        sl = slice(hh * HEAD_SLOT, (hh + 1) * HEAD_SLOT)
        s = lax.dot_general(q_ref[0, :, sl], k_ref[0, :, sl], (((1,), (1,)), ((), ())),
                            preferred_element_type=F32)
        m = jnp.max(s, axis=-1, keepdims=True)
        p = jnp.exp(s - m).astype(BF16)
        o = _dot(p, v_ref[0, :, sl])
        outs.append(o / pltpu.roll(o, HEAD_SLOT // 2, 1))
    lane = lax.broadcasted_iota(jnp.int32, (tq, HEAD_SLOT), 1)
    o_ref[0] = jnp.where(lane < V_DIM, outs[0], outs[1]).astype(o_ref.dtype)


def _attention(q, k, v, tq):
    B, S, _ = q.shape
    pair = 2 * HEAD_SLOT
    return pl.pallas_call(
        _attn_kernel,
        grid=(B, N_HEADS // 2, S // tq),
        in_specs=[pl.BlockSpec((1, tq, pair), lambda b, h, i: (b, i, h)),
                  pl.BlockSpec((1, S, pair), lambda b, h, i: (b, 0, h)),
                  pl.BlockSpec((1, S, pair), lambda b, h, i: (b, 0, h))],
        out_specs=pl.BlockSpec((1, tq, 2 * V_DIM), lambda b, h, i: (b, i, h)),
        out_shape=jax.ShapeDtypeStruct((B, S, N_HEADS * V_DIM), BF16),
        compiler_params=pltpu.CompilerParams(
            dimension_semantics=("arbitrary", "arbitrary", "arbitrary"), vmem_limit_bytes=VMEM_LIMIT_BYTES),
        name="attention",
    )(q, k, v)


def _post_kernel(pool_ref, prev_ref, next_ref, yatt_ref, x_ref, p_ref, w_pool_ref, pscale_ref, w_op_ref,
                 w_oa_ref, ln2_ref, w_gate_ref, w_up_ref, w_down_ref, plen_ref, w_pg_ref, w_pp_ref,
                 o_ref, ext_ref, *, seq_len):
    tm = pool_ref.shape[1]
    j = pl.program_id(1)
    cur = pool_ref[0]
    ext_ref[0:POOL_HALO] = jnp.where(j > 0, prev_ref[0], 0.0)
    ext_ref[POOL_HALO:POOL_HALO + tm] = cur
    ext_ref[POOL_HALO + tm:2 * POOL_HALO + tm] = jnp.where(j < pl.num_programs(1) - 1, next_ref[0], 0.0)

    t = j * tm + lax.broadcasted_iota(jnp.int32, (tm, 1), 0)
    ys = []
    for g, w in enumerate(POOL_WINDOWS):
        sl = slice(g * POOL_GC, (g + 1) * POOL_GC)
        acc = ext_ref[pl.ds(POOL_HALO - w // 2, tm), sl]
        for d in range(1 - w // 2, w // 2):
            acc = acc + ext_ref[pl.ds(POOL_HALO + d, tm), sl]
        lo = jnp.clip(t - w // 2, 0, seq_len)
        hi = jnp.clip(t - w // 2 + w, 0, seq_len)
        mean = acc / (hi - lo).astype(F32)
        ys.append(_dot((mean - cur[:, sl]).astype(BF16), w_pool_ref[g]))
    y_pool = (jnp.concatenate(ys, axis=1) * pscale_ref[...]).astype(BF16)

    h = x_ref[0] + _dot(y_pool, w_op_ref[...]) + _dot(yatt_ref[0], w_oa_ref[...])

    u2 = _rms(h, ln2_ref[...], D_MODEL).astype(BF16)
    act = (jax.nn.silu(_dot(u2, w_gate_ref[...])) * _dot(u2, w_up_ref[...])).astype(BF16)
    h = h + _dot(act, w_down_ref[...])

    gate = jax.nn.sigmoid(_dot(_rms(h, plen_ref[...], D_MODEL).astype(BF16), w_pg_ref[...]))
    o_ref[0] = h + gate * _dot(p_ref[0].astype(BF16), w_pp_ref[...])


def _post(pool_in, y_att, x, p, wts, tm):
    B, S, _ = x.shape
    nj = S // tm
    hb = tm // POOL_HALO
    row = lambda w: pl.BlockSpec((1, tm, w), lambda b, j: (b, j, 0))
    prev = pl.BlockSpec((1, POOL_HALO, POOL_W), lambda b, j: (b, jnp.maximum(j * hb - 1, 0), 0))
    nxt = pl.BlockSpec((1, POOL_HALO, POOL_W),
                       lambda b, j: (b, jnp.minimum((j + 1) * hb, S // POOL_HALO - 1), 0))
    consts = [wts[k] for k in ("w_pool", "pool_scale", "w_o_pool", "w_o_att", "ln2", "w_gate", "w_up",
                               "w_down", "ple_norm", "w_ple_gate", "w_ple_proj")]
    return pl.pallas_call(
        functools.partial(_post_kernel, seq_len=S),
        grid=(B, nj),
        in_specs=[row(POOL_W), prev, nxt, row(N_HEADS * V_DIM), row(D_MODEL), row(PLE_DIM)]
        + [_const_spec(a.shape) for a in consts],
        out_specs=row(D_MODEL),
        out_shape=jax.ShapeDtypeStruct((B, S, D_MODEL), F32),
        scratch_shapes=[pltpu.VMEM((tm + 2 * POOL_HALO, POOL_W), F32)],
        compiler_params=pltpu.CompilerParams(
            dimension_semantics=("arbitrary", "arbitrary"), vmem_limit_bytes=VMEM_LIMIT_BYTES),
        name="post",
    )(pool_in, pool_in, pool_in, y_att, x, p, *consts)


def _rope_tables(S):
    inv = ROPE_BASE ** (-jnp.arange(0, QK_ROPE, 2, dtype=F32) / QK_ROPE)
    ang = jnp.arange(S, dtype=F32)[:, None] * inv[None, :]
    cos, sin = jnp.cos(ang), jnp.sin(ang)
    half = QK_ROPE // 2
    zeros = lambda n: jnp.zeros((S, n), F32)
    c = jnp.concatenate([jnp.ones((S, QK_NOPE), F32), cos, cos, zeros(HEAD_SLOT - QK_DIM)], axis=1)
    s1 = jnp.concatenate([zeros(QK_NOPE), -sin, zeros(HEAD_SLOT - QK_NOPE - half)], axis=1)
    s2 = jnp.concatenate([zeros(QK_NOPE + half), sin, zeros(HEAD_SLOT - QK_DIM)], axis=1)
    return c, s1, s2


def _prepare_weights(ln1, w_in, w_pool, pool_scale, q_a_norm, w_qb, kv_a_norm, w_kvb, q_norm, k_norm, w_o,
                     ln2, w_gate, w_up, w_down, ple_norm, w_ple_gate, w_ple_proj):
    row = lambda a: a.reshape(1, -1).astype(F32)
    split = POOL_W + Q_LORA + KV_LORA
    w_in_pad = jnp.zeros((D_MODEL, IN_W_PAD), F32)
    w_in_pad = w_in_pad.at[:, :split].set(w_in[:, :split])
    w_in_pad = w_in_pad.at[:, split + QK_NOPE:split + QK_DIM].set(w_in[:, split:])
    w_qb_pad = jnp.pad(w_qb.reshape(Q_LORA, N_HEADS, QK_DIM), ((0, 0), (0, 0), (0, HEAD_SLOT - QK_DIM)))
    kvb = w_kvb.reshape(KV_LORA, N_HEADS, QK_NOPE + V_DIM)
    w_kn = jnp.pad(kvb[..., :QK_NOPE], ((0, 0), (0, 0), (0, HEAD_SLOT - QK_NOPE)))
    wv = kvb[..., QK_NOPE:]
    even = (jnp.arange(N_HEADS) % 2 == 0)[None, :, None]
    zv = jnp.zeros_like(wv)
    w_v = jnp.where(even, jnp.concatenate([wv, zv], -1), jnp.concatenate([zv, wv], -1))
    lane = jnp.arange(HEAD_SLOT)[None, :]
    v_one = jnp.where(even[0], lane >= V_DIM, lane < V_DIM).astype(F32).reshape(1, QK_SLOTS)
    gpad = lambda g: jnp.pad(g, (0, HEAD_SLOT - QK_DIM)).reshape(1, HEAD_SLOT).astype(F32)
    return {
        "ln1": row(ln1), "w_in": w_in_pad.astype(BF16), "q_a_norm": row(q_a_norm),
        "w_qb": w_qb_pad.reshape(Q_LORA, QK_SLOTS).astype(BF16), "kv_a_norm": row(kv_a_norm),
        "w_kn": w_kn.reshape(KV_LORA, QK_SLOTS).astype(BF16),
        "w_v": w_v.reshape(KV_LORA, QK_SLOTS).astype(BF16), "v_one": v_one,
        "q_norm": gpad(q_norm), "k_norm": gpad(k_norm),
        "w_pool": w_pool.astype(BF16), "pool_scale": row(pool_scale),
        "w_o_pool": w_o[:POOL_W].astype(BF16), "w_o_att": w_o[POOL_W:].astype(BF16),
        "ln2": row(ln2), "w_gate": w_gate.astype(BF16), "w_up": w_up.astype(BF16),
        "w_down": w_down.astype(BF16), "ple_norm": row(ple_norm),
        "w_ple_gate": w_ple_gate.astype(BF16), "w_ple_proj": w_ple_proj.astype(BF16),
    }


def _layer(x, p, wts, tables):
    pool_in, q, k, v = _projection(x, wts, tables, tm=512)
    y_att = _attention(q, k, v, tq=512)
    return _post(pool_in, y_att, x, p, wts, tm=256)


def kernel(x_prompt, x_sample, p_prompt, p_sample, ln1, w_in, w_pool, pool_scale, q_a_norm, w_qb, kv_a_norm,
           w_kvb, q_norm, k_norm, w_o, ln2, w_gate, w_up, w_down, ple_norm, w_ple_gate, w_ple_proj):
    params = (ln1, w_in, w_pool, pool_scale, q_a_norm, w_qb, kv_a_norm, w_kvb, q_norm, k_norm, w_o, ln2,
              w_gate, w_up, w_down, ple_norm, w_ple_gate, w_ple_proj)
    wts = _prepare_weights(*[a[0] for a in params])
    tables = _rope_tables(x_prompt.shape[1])
    y_prompt = _layer(x_prompt, p_prompt[0], wts, tables)
    y_sample = _layer(x_sample, p_sample[0], wts, tables)
    return (y_prompt, y_sample)
```

```python
import functools
import math

import jax
import jax.numpy as jnp
from jax import lax
from jax.experimental import pallas as pl
from jax.experimental.pallas import tpu as pltpu

D_MODEL = 1024
POOL_W = 512
POOL_WINDOWS = (2, 4, 8, 16)
POOL_GC = 128
N_HEADS = 8
QK_NOPE = 64
QK_ROPE = 32
QK_DIM = QK_NOPE + QK_ROPE
V_DIM = 64
Q_LORA = 384
KV_LORA = 256
ROPE_BASE = 10000.0
ATTN_SCALE = 1.0 / math.sqrt(QK_DIM)
D_FF = 2816
PLE_DIM = 256
EPS = 1e-6

LANES = 128
SUBLANES = 8
HEAD_SLOT = LANES
HALF_ROPE = QK_ROPE // 2
QK_SLOTS = N_HEADS * HEAD_SLOT
IN_W_PAD = POOL_W + Q_LORA + KV_LORA + HEAD_SLOT
POOL_HALO = SUBLANES
VMEM_LIMIT_BYTES = 60 * 1024 * 1024

F32 = jnp.float32
BF16 = jnp.bfloat16


def _dot(a, b):
    return jnp.dot(a, b, preferred_element_type=F32)


def _rms(x, g, n):
    ms = jnp.sum(x * x, axis=-1, keepdims=True) * (1.0 / n)
    return x * lax.rsqrt(ms + EPS) * g


def _const_spec(shape):
    return pl.BlockSpec(shape, lambda *_: (0,) * len(shape), pipeline_mode=pl.Buffered(1))


def _proj_kernel(x_ref, ln1_ref, w_in_ref, qa_ref, w_qb_ref, kva_ref, w_kn_ref, w_v_ref, vone_ref,
                 kg_ref, tq_ref, ta_ref, tb_ref, pool_ref, q_ref, k_ref, v_ref, *, sub):
    tm = x_ref.shape[1]
    lane = lax.broadcasted_iota(jnp.int32, (sub, HEAD_SLOT), 1)
    for r0 in range(0, tm, sub):
        rows = slice(r0, r0 + sub)
        u = _rms(x_ref[0, rows], ln1_ref[...], D_MODEL).astype(BF16)
        z = _dot(u, w_in_ref[...])
        pool_ref[0, rows] = z[:, :POOL_W]
        cq = _rms(z[:, POOL_W:POOL_W + Q_LORA], qa_ref[...], Q_LORA).astype(BF16)
        ckv = _rms(z[:, POOL_W + Q_LORA:POOL_W + Q_LORA + KV_LORA], kva_ref[...], KV_LORA).astype(BF16)
        zr = z[:, POOL_W + Q_LORA + KV_LORA:]
        k_rope = zr * ta_ref[rows] + pltpu.roll(zr, HEAD_SLOT // 2, 1) * tb_ref[rows]
        ss_rope = jnp.sum(zr * zr, axis=-1, keepdims=True) * 0.25
        qf = _dot(cq, w_qb_ref[...])
        kn = _dot(ckv, w_kn_ref[...])
        vv = _dot(ckv, w_v_ref[...])
        v_ref[0, rows] = (vv + vone_ref[...]).astype(BF16)
        tq = tq_ref[rows]
        for h in range(N_HEADS):
            sl = slice(h * HEAD_SLOT, (h + 1) * HEAD_SLOT)
            xq = qf[:, sl]
            xq96 = jnp.where(lane < QK_DIM, xq, 0.0)
            rq = lax.rsqrt(jnp.sum(xq96 * xq96, axis=-1, keepdims=True) * (1.0 / QK_DIM) + EPS)
            q_ref[0, rows, sl] = (xq * rq * tq).astype(BF16)
            xk = kn[:, sl]
            rk = lax.rsqrt((jnp.sum(xk * xk, axis=-1, keepdims=True) + ss_rope) * (1.0 / QK_DIM) + EPS)
            k_ref[0, rows, sl] = ((xk * kg_ref[...] + k_rope) * rk).astype(BF16)


def _projection(x, wts, tables, tm, sub):
    B, S, _ = x.shape
    nj = S // tm
    row = lambda w: pl.BlockSpec((1, tm, w), lambda b, j: (b, j, 0))
    tab = pl.BlockSpec((tm, HEAD_SLOT), lambda b, j: (j, 0))
    consts = [wts[k] for k in ("ln1", "w_in", "q_a_norm", "w_qb", "kv_a_norm", "w_kn", "w_v", "v_one",
                               "k_gain")]
    return pl.pallas_call(
        functools.partial(_proj_kernel, sub=sub),
        grid=(B, nj),
        in_specs=[row(D_MODEL)] + [_const_spec(a.shape) for a in consts] + [tab, tab, tab],
        out_specs=[row(POOL_W), row(QK_SLOTS), row(QK_SLOTS), row(QK_SLOTS)],
        out_shape=[jax.ShapeDtypeStruct((B, S, POOL_W), F32)]
        + [jax.ShapeDtypeStruct((B, S, QK_SLOTS), BF16)] * 3,
        compiler_params=pltpu.CompilerParams(
            dimension_semantics=("arbitrary", "arbitrary"), vmem_limit_bytes=VMEM_LIMIT_BYTES),
        name="projection",
    )(x, *consts, *tables)


def _attn_step(q_ref, k_ref, v_ref, o_ref, s_new_ref, s_old_ref):
    tq = q_ref.shape[1]
    for hh in range(2):
        sl = slice(hh * HEAD_SLOT, (hh + 1) * HEAD_SLOT)
        s_new_ref[hh] = lax.dot_general(q_ref[0, :, sl], k_ref[0, :, sl], (((1,), (1,)), ((), ())),
                                        preferred_element_type=F32)
    outs = []
    for hh in range(2):
        sl = slice(hh * HEAD_SLOT, (hh + 1) * HEAD_SLOT)
        s = s_old_ref[hh]
        p = jnp.exp2(s - jnp.max(s, axis=-1, keepdims=True)).astype(BF16)
        o = _dot(p, v_ref[0, :, sl])
        outs.append(o / pltpu.roll(o, HEAD_SLOT // 2, 1))
    lane = lax.broadcasted_iota(jnp.int32, (tq, HEAD_SLOT), 1)
    o_ref[0] = jnp.where(lane < V_DIM, outs[0], outs[1]).astype(o_ref.dtype)


def _attn_kernel(q_ref, k_ref, v_ref, o_ref, s_even_ref, s_odd_ref):
    t = pl.program_id(0)

    @pl.when(t == 0)
    def _():
        s_odd_ref[...] = jnp.zeros_like(s_odd_ref)

    @pl.when(t % 2 == 0)
    def _():
        _attn_step(q_ref, k_ref, v_ref, o_ref, s_even_ref, s_odd_ref)

    @pl.when(t % 2 == 1)
    def _():
        _attn_step(q_ref, k_ref, v_ref, o_ref, s_odd_ref, s_even_ref)


def _attention(q, k, v, tq):
    B, S, _ = q.shape
    pair = 2 * HEAD_SLOT
    n_pair = N_HEADS // 2
    nq = S // tq
    n_blocks = B * n_pair * nq

    def unravel(t):
        return t // (n_pair * nq), (t // nq) % n_pair, t % nq

    def q_map(t):
        b, h, i = unravel(jnp.minimum(t, n_blocks - 1))
        return b, i, h

    def k_map(t):
        b, h, _ = unravel(jnp.minimum(t, n_blocks - 1))
        return b, 0, h

    def v_map(t):
        b, h, _ = unravel(jnp.maximum(t - 1, 0))
        return b, 0, h

    def o_map(t):
        b, h, i = unravel(jnp.maximum(t - 1, 0))
        return b, i, h

    return pl.pallas_call(
        _attn_kernel,
        grid=(n_blocks + 1,),
        in_specs=[pl.BlockSpec((1, tq, pair), q_map),
                  pl.BlockSpec((1, S, pair), k_map),
                  pl.BlockSpec((1, S, pair), v_map)],
        out_specs=pl.BlockSpec((1, tq, 2 * V_DIM), o_map),
        out_shape=jax.ShapeDtypeStruct((B, S, N_HEADS * V_DIM), BF16),
        scratch_shapes=[pltpu.VMEM((2, tq, S), F32), pltpu.VMEM((2, tq, S), F32)],
        compiler_params=pltpu.CompilerParams(
            dimension_semantics=("arbitrary",), vmem_limit_bytes=VMEM_LIMIT_BYTES),
        name="attention",
    )(q, k, v)


def _post_kernel(pool_ref, prev_ref, next_ref, yatt_ref, x_ref, p_ref, w_pool_ref, pscale_ref, w_op_ref,
                 w_oa_ref, ln2_ref, w_gate_ref, w_up_ref, w_down_ref, plen_ref, w_pg_ref, w_pp_ref,
                 o_ref, ext_ref, *, seq_len):
    tm = pool_ref.shape[1]
    j = pl.program_id(1)
    cur = pool_ref[0]
    ext_ref[0:POOL_HALO] = jnp.where(j > 0, prev_ref[0], 0.0)
    ext_ref[POOL_HALO:POOL_HALO + tm] = cur
    ext_ref[POOL_HALO + tm:2 * POOL_HALO + tm] = jnp.where(j < pl.num_programs(1) - 1, next_ref[0], 0.0)

    t = j * tm + lax.broadcasted_iota(jnp.int32, (tm, 1), 0)
    ys = []
    for g, w in enumerate(POOL_WINDOWS):
        sl = slice(g * POOL_GC, (g + 1) * POOL_GC)
        acc = ext_ref[pl.ds(POOL_HALO - w // 2, tm), sl]
        for d in range(1 - w // 2, w // 2):
            acc = acc + ext_ref[pl.ds(POOL_HALO + d, tm), sl]
        lo = jnp.clip(t - w // 2, 0, seq_len)
        hi = jnp.clip(t - w // 2 + w, 0, seq_len)
        mean = acc / (hi - lo).astype(F32)
        ys.append(_dot((mean - cur[:, sl]).astype(BF16), w_pool_ref[g]))
    y_pool = (jnp.concatenate(ys, axis=1) * pscale_ref[...]).astype(BF16)

    h = x_ref[0] + _dot(y_pool, w_op_ref[...]) + _dot(yatt_ref[0], w_oa_ref[...])

    u2 = _rms(h, ln2_ref[...], D_MODEL).astype(BF16)
    act = (jax.nn.silu(_dot(u2, w_gate_ref[...])) * _dot(u2, w_up_ref[...])).astype(BF16)
    h = h + _dot(act, w_down_ref[...])

    gate = jax.nn.sigmoid(_dot(_rms(h, plen_ref[...], D_MODEL).astype(BF16), w_pg_ref[...]))
    o_ref[0] = h + gate * _dot(p_ref[0].astype(BF16), w_pp_ref[...])


def _post(pool_in, y_att, x, p, wts, tm):
    B, S, _ = x.shape
    nj = S // tm
    hb = tm // POOL_HALO
    row = lambda w: pl.BlockSpec((1, tm, w), lambda b, j: (b, j, 0))
    prev = pl.BlockSpec((1, POOL_HALO, POOL_W), lambda b, j: (b, jnp.maximum(j * hb - 1, 0), 0))
    nxt = pl.BlockSpec((1, POOL_HALO, POOL_W),
                       lambda b, j: (b, jnp.minimum((j + 1) * hb, S // POOL_HALO - 1), 0))
    consts = [wts[k] for k in ("w_pool", "pool_scale", "w_o_pool", "w_o_att", "ln2", "w_gate", "w_up",
                               "w_down", "ple_norm", "w_ple_gate", "w_ple_proj")]
    return pl.pallas_call(
        functools.partial(_post_kernel, seq_len=S),
        grid=(B, nj),
        in_specs=[row(POOL_W), prev, nxt, row(N_HEADS * V_DIM), row(D_MODEL), row(PLE_DIM)]
        + [_const_spec(a.shape) for a in consts],
        out_specs=row(D_MODEL),
        out_shape=jax.ShapeDtypeStruct((B, S, D_MODEL), F32),
        scratch_shapes=[pltpu.VMEM((tm + 2 * POOL_HALO, POOL_W), F32)],
        compiler_params=pltpu.CompilerParams(
            dimension_semantics=("arbitrary", "arbitrary"), vmem_limit_bytes=VMEM_LIMIT_BYTES),
        name="post",
    )(pool_in, pool_in, pool_in, y_att, x, p, *consts)


def _rope_tables(S, q_norm, k_norm):
    inv = ROPE_BASE ** (-jnp.arange(0, QK_ROPE, 2, dtype=F32) / QK_ROPE)
    ang = jnp.arange(S, dtype=F32)[:, None] * inv[None, :]
    cos, sin = jnp.cos(ang), jnp.sin(ang)
    nope = lambda g: jnp.broadcast_to(g[None, :QK_NOPE], (S, QK_NOPE))
    g1 = lambda g: g[None, QK_NOPE:QK_NOPE + HALF_ROPE]
    g2 = lambda g: g[None, QK_NOPE + HALF_ROPE:]
    tq = jnp.concatenate([nope(q_norm), g1(q_norm) * cos, g2(q_norm) * cos, -g2(q_norm) * sin, g1(q_norm) * sin],
                         axis=1) * (ATTN_SCALE * math.log2(math.e))
    zeros = jnp.zeros((S, QK_NOPE), F32)
    ta = jnp.concatenate([zeros] + [g1(k_norm) * cos, g2(k_norm) * cos] * 2, axis=1)
    tb = jnp.concatenate([zeros] + [-g2(k_norm) * sin, g1(k_norm) * sin] * 2, axis=1)
    return tq, ta, tb


def _prepare_weights(ln1, w_in, w_pool, pool_scale, q_a_norm, w_qb, kv_a_norm, w_kvb, q_norm, k_norm, w_o,
                     ln2, w_gate, w_up, w_down, ple_norm, w_ple_gate, w_ple_proj):
    row = lambda a: a.reshape(1, -1).astype(F32)
    split = POOL_W + Q_LORA + KV_LORA
    k1, k2 = w_in[:, split:split + HALF_ROPE], w_in[:, split + HALF_ROPE:]
    w_in_pad = jnp.concatenate([w_in[:, :split], k2, k1, k2, k1, k1, k2, k1, k2], axis=1)
    qb = w_qb.reshape(Q_LORA, N_HEADS, QK_DIM)
    q1, q2 = qb[..., QK_NOPE:QK_NOPE + HALF_ROPE], qb[..., QK_NOPE + HALF_ROPE:]
    w_qb_pad = jnp.concatenate([qb[..., :QK_NOPE], q1, q2, q2, q1], axis=-1)
    kvb = w_kvb.reshape(KV_LORA, N_HEADS, QK_NOPE + V_DIM)
    w_kn = jnp.pad(kvb[..., :QK_NOPE], ((0, 0), (0, 0), (0, HEAD_SLOT - QK_NOPE)))
    wv = kvb[..., QK_NOPE:]
    even = (jnp.arange(N_HEADS) % 2 == 0)[None, :, None]
    zv = jnp.zeros_like(wv)
    w_v = jnp.where(even, jnp.concatenate([wv, zv], -1), jnp.concatenate([zv, wv], -1))
    lane = jnp.arange(HEAD_SLOT)[None, :]
    v_one = jnp.where(even[0], lane >= V_DIM, lane < V_DIM).astype(F32).reshape(1, QK_SLOTS)
    k_gain = jnp.pad(k_norm[:QK_NOPE], (0, HEAD_SLOT - QK_NOPE)).reshape(1, HEAD_SLOT).astype(F32)
    return {
        "ln1": row(ln1), "w_in": w_in_pad.astype(BF16), "q_a_norm": row(q_a_norm),
        "w_qb": w_qb_pad.reshape(Q_LORA, QK_SLOTS).astype(BF16), "kv_a_norm": row(kv_a_norm),
        "w_kn": w_kn.reshape(KV_LORA, QK_SLOTS).astype(BF16),
        "w_v": w_v.reshape(KV_LORA, QK_SLOTS).astype(BF16), "v_one": v_one, "k_gain": k_gain,
        "w_pool": w_pool.astype(BF16), "pool_scale": row(pool_scale),
        "w_o_pool": w_o[:POOL_W].astype(BF16), "w_o_att": w_o[POOL_W:].astype(BF16),
        "ln2": row(ln2), "w_gate": w_gate.astype(BF16), "w_up": w_up.astype(BF16),
        "w_down": w_down.astype(BF16), "ple_norm": row(ple_norm),
        "w_ple_gate": w_ple_gate.astype(BF16), "w_ple_proj": w_ple_proj.astype(BF16),
    }


def _layer(x, p, wts, tables):
    pool_in, q, k, v = _projection(x, wts, tables, tm=512, sub=128)
    y_att = _attention(q, k, v, tq=512)
    return _post(pool_in, y_att, x, p, wts, tm=256)


def kernel(x_prompt, x_sample, p_prompt, p_sample, ln1, w_in, w_pool, pool_scale, q_a_norm, w_qb, kv_a_norm,
           w_kvb, q_norm, k_norm, w_o, ln2, w_gate, w_up, w_down, ple_norm, w_ple_gate, w_ple_proj):
    params = (ln1, w_in, w_pool, pool_scale, q_a_norm, w_qb, kv_a_norm, w_kvb, q_norm, k_norm, w_o, ln2,
              w_gate, w_up, w_down, ple_norm, w_ple_gate, w_ple_proj)
    wts = _prepare_weights(*[a[0] for a in params])
    tables = _rope_tables(x_prompt.shape[1], q_norm[0], k_norm[0])
    y_prompt = _layer(x_prompt, p_prompt[0], wts, tables)
    y_sample = _layer(x_sample, p_sample[0], wts, tables)
    return (y_prompt, y_sample)
```

```python
import functools
import math

import jax
import jax.numpy as jnp
from jax import lax
from jax.experimental import pallas as pl
from jax.experimental.pallas import tpu as pltpu

D_MODEL = 1024
POOL_W = 512
POOL_WINDOWS = (2, 4, 8, 16)
POOL_GC = 128
N_HEADS = 8
QK_NOPE = 64
QK_ROPE = 32
QK_DIM = QK_NOPE + QK_ROPE
V_DIM = 64
Q_LORA = 384
KV_LORA = 256
ROPE_BASE = 10000.0
ATTN_SCALE = 1.0 / math.sqrt(QK_DIM)
D_FF = 2816
PLE_DIM = 256
EPS = 1e-6

LANES = 128
SUBLANES = 8
HEAD_SLOT = LANES
HALF_ROPE = QK_ROPE // 2
QK_SLOTS = N_HEADS * HEAD_SLOT
IN_W_PAD = POOL_W + Q_LORA + KV_LORA + HEAD_SLOT
POOL_HALO = SUBLANES
VMEM_LIMIT_BYTES = 60 * 1024 * 1024

F32 = jnp.float32
BF16 = jnp.bfloat16


def _dot(a, b):
    return jnp.dot(a, b, preferred_element_type=F32)


def _rms(x, g, n):
    ms = jnp.sum(x * x, axis=-1, keepdims=True) * (1.0 / n)
    return x * lax.rsqrt(ms + EPS) * g


def _const_spec(shape):
    return pl.BlockSpec(shape, lambda *_: (0,) * len(shape), pipeline_mode=pl.Buffered(1))


def _proj_kernel(x_ref, ln1_ref, w_in_ref, qa_ref, w_qb_ref, kva_ref, w_kn_ref, w_v_ref, vone_ref,
                 kg_ref, tq_ref, ta_ref, tb_ref, pool_ref, q_ref, k_ref, v_ref, *, sub):
    tm = x_ref.shape[1]
    lane = lax.broadcasted_iota(jnp.int32, (sub, HEAD_SLOT), 1)
    for r0 in range(0, tm, sub):
        rows = slice(r0, r0 + sub)
        u = _rms(x_ref[0, rows], ln1_ref[...], D_MODEL).astype(BF16)
        z = _dot(u, w_in_ref[...])
        pool_ref[0, rows] = z[:, :POOL_W]
        cq = _rms(z[:, POOL_W:POOL_W + Q_LORA], qa_ref[...], Q_LORA).astype(BF16)
        ckv = _rms(z[:, POOL_W + Q_LORA:POOL_W + Q_LORA + KV_LORA], kva_ref[...], KV_LORA).astype(BF16)
        zr = z[:, POOL_W + Q_LORA + KV_LORA:]
        k_rope = zr * ta_ref[rows] + pltpu.roll(zr, HEAD_SLOT // 2, 1) * tb_ref[rows]
        ss_rope = jnp.sum(zr * zr, axis=-1, keepdims=True) * 0.25
        qf = _dot(cq, w_qb_ref[...])
        kn = _dot(ckv, w_kn_ref[...])
        vv = _dot(ckv, w_v_ref[...])
        v_ref[0, rows] = (vv + vone_ref[...]).astype(BF16)
        tq = tq_ref[rows]
        for h in range(N_HEADS):
            sl = slice(h * HEAD_SLOT, (h + 1) * HEAD_SLOT)
            xq = qf[:, sl]
            xq96 = jnp.where(lane < QK_DIM, xq, 0.0)
            rq = lax.rsqrt(jnp.sum(xq96 * xq96, axis=-1, keepdims=True) * (1.0 / QK_DIM) + EPS)
            q_ref[0, rows, sl] = (xq * rq * tq).astype(BF16)
            xk = kn[:, sl]
            rk = lax.rsqrt((jnp.sum(xk * xk, axis=-1, keepdims=True) + ss_rope) * (1.0 / QK_DIM) + EPS)
            k_ref[0, rows, sl] = ((xk * kg_ref[...] + k_rope) * rk).astype(BF16)


def _projection(x, wts, tables, tm, sub):
    B, S, _ = x.shape
    nj = S // tm
    row = lambda w: pl.BlockSpec((1, tm, w), lambda b, j: (b, j, 0))
    tab = pl.BlockSpec((tm, HEAD_SLOT), lambda b, j: (j, 0))
    consts = [wts[k] for k in ("ln1", "w_in", "q_a_norm", "w_qb", "kv_a_norm", "w_kn", "w_v", "v_one",
                               "k_gain")]
    return pl.pallas_call(
        functools.partial(_proj_kernel, sub=sub),
        grid=(B, nj),
        in_specs=[row(D_MODEL)] + [_const_spec(a.shape) for a in consts] + [tab, tab, tab],
        out_specs=[row(POOL_W), row(QK_SLOTS), row(QK_SLOTS), row(QK_SLOTS)],
        out_shape=[jax.ShapeDtypeStruct((B, S, POOL_W), F32)]
        + [jax.ShapeDtypeStruct((B, S, QK_SLOTS), BF16)] * 3,
        compiler_params=pltpu.CompilerParams(
            dimension_semantics=("arbitrary", "arbitrary"), vmem_limit_bytes=VMEM_LIMIT_BYTES),
        name="projection",
    )(x, *consts, *tables)


def _attn_step(q_ref, k_ref, v_ref, o_ref, new_refs, old_refs):
    s_new_ref, m_new_ref = new_refs
    s_old_ref, m_old_ref = old_refs
    tq = q_ref.shape[1]
    heads = s_new_ref.shape[0]
    for hh in range(heads):
        sl = slice(hh * HEAD_SLOT, (hh + 1) * HEAD_SLOT)
        s = lax.dot_general(q_ref[0, :, sl], k_ref[0, :, sl], (((1,), (1,)), ((), ())),
                            preferred_element_type=F32)
        s_new_ref[hh] = s
        m_new_ref[hh] = jnp.max(s, axis=-1, keepdims=True)
    lane = lax.broadcasted_iota(jnp.int32, (tq, HEAD_SLOT), 1)
    outs = []
    for hh in range(heads):
        sl = slice(hh * HEAD_SLOT, (hh + 1) * HEAD_SLOT)
        p = jnp.exp2(s_old_ref[hh] - m_old_ref[hh]).astype(BF16)
        o = _dot(p, v_ref[0, :, sl])
        outs.append(o / pltpu.roll(o, HEAD_SLOT // 2, 1))
    for pr in range(heads // 2):
        o_ref[0, :, pr * HEAD_SLOT:(pr + 1) * HEAD_SLOT] = jnp.where(
            lane < V_DIM, outs[2 * pr], outs[2 * pr + 1]).astype(o_ref.dtype)


def _attn_kernel(q_ref, k_ref, v_ref, o_ref, s_even_ref, m_even_ref, s_odd_ref, m_odd_ref):
    t = pl.program_id(0)
    even, odd = (s_even_ref, m_even_ref), (s_odd_ref, m_odd_ref)

    @pl.when(t == 0)
    def _():
        s_odd_ref[...] = jnp.zeros_like(s_odd_ref)
        m_odd_ref[...] = jnp.zeros_like(m_odd_ref)

    @pl.when(t % 2 == 0)
    def _():
        _attn_step(q_ref, k_ref, v_ref, o_ref, even, odd)

    @pl.when(t % 2 == 1)
    def _():
        _attn_step(q_ref, k_ref, v_ref, o_ref, odd, even)


def _attention(q, k, v, tq, heads):
    B, S, _ = q.shape
    width = heads * HEAD_SLOT
    n_grp = N_HEADS // heads
    nq = S // tq
    n_blocks = B * n_grp * nq

    def unravel(t):
        return t // (n_grp * nq), (t // nq) % n_grp, t % nq

    def q_map(t):
        b, h, i = unravel(jnp.minimum(t, n_blocks - 1))
        return b, i, h

    def k_map(t):
        b, h, _ = unravel(jnp.minimum(t, n_blocks - 1))
        return b, 0, h

    def v_map(t):
        b, h, _ = unravel(jnp.maximum(t - 1, 0))
        return b, 0, h

    def o_map(t):
        b, h, i = unravel(jnp.maximum(t - 1, 0))
        return b, i, h

    return pl.pallas_call(
        _attn_kernel,
        grid=(n_blocks + 1,),
        in_specs=[pl.BlockSpec((1, tq, width), q_map),
                  pl.BlockSpec((1, S, width), k_map),
                  pl.BlockSpec((1, S, width), v_map)],
        out_specs=pl.BlockSpec((1, tq, heads * V_DIM), o_map),
        out_shape=jax.ShapeDtypeStruct((B, S, N_HEADS * V_DIM), BF16),
        scratch_shapes=[pltpu.VMEM((heads, tq, S), F32), pltpu.VMEM((heads, tq, 1), F32)] * 2,
        compiler_params=pltpu.CompilerParams(
            dimension_semantics=("arbitrary",), vmem_limit_bytes=VMEM_LIMIT_BYTES),
        name="attention",
    )(q, k, v)


def _post_kernel(pool_ref, prev_ref, next_ref, inv_ref, yatt_ref, x_ref, p_ref, w_pool_ref, pscale_ref, w_op_ref,
                 w_oa_ref, ln2_ref, w_gate_ref, w_up_ref, w_down_ref, plen_ref, w_pg_ref, w_pp_ref, o_ref):
    tm = pool_ref.shape[1]
    j = pl.program_id(1)
    ext = jnp.concatenate([jnp.where(j > 0, prev_ref[0], 0.0), pool_ref[0],
                           jnp.where(j < pl.num_programs(1) - 1, next_ref[0], 0.0)], axis=0)
    n_ext = tm + 2 * POOL_HALO
    ahead_by = lambda a, s: pltpu.roll(a, n_ext - s, 0)
    ys = []
    for g, w in enumerate(POOL_WINDOWS):
        sl = slice(g * POOL_GC, (g + 1) * POOL_GC)
        acc = ext[:, sl]
        span = 1
        while span < w:
            acc = acc + ahead_by(acc, span)
            span *= 2
        acc = ahead_by(acc, POOL_HALO - w // 2)[0:tm] if w // 2 < POOL_HALO else acc[0:tm]
        diff = acc * inv_ref[:, sl] - pool_ref[0, :, sl]
        ys.append(_dot(diff.astype(BF16), w_pool_ref[g]))
    y_pool = (jnp.concatenate(ys, axis=1) * pscale_ref[...]).astype(BF16)

    h = x_ref[0] + _dot(y_pool, w_op_ref[...]) + _dot(yatt_ref[0], w_oa_ref[...])

    u2 = _rms(h, ln2_ref[...], D_MODEL).astype(BF16)
    act = (jax.nn.silu(_dot(u2, w_gate_ref[...])) * _dot(u2, w_up_ref[...])).astype(BF16)
    h = h + _dot(act, w_down_ref[...])

    gate = jax.nn.sigmoid(_dot(_rms(h, plen_ref[...], D_MODEL).astype(BF16), w_pg_ref[...]))
    o_ref[0] = h + gate * _dot(p_ref[0].astype(BF16), w_pp_ref[...])


def _post(pool_in, y_att, x, p, wts, inv_cnt, tm):
    B, S, _ = x.shape
    nj = S // tm
    hb = tm // POOL_HALO
    row = lambda w: pl.BlockSpec((1, tm, w), lambda b, j: (b, j, 0))
    prev = pl.BlockSpec((1, POOL_HALO, POOL_W), lambda b, j: (b, jnp.maximum(j * hb - 1, 0), 0))
    nxt = pl.BlockSpec((1, POOL_HALO, POOL_W),
                       lambda b, j: (b, jnp.minimum((j + 1) * hb, S // POOL_HALO - 1), 0))
    inv = pl.BlockSpec((tm, POOL_W), lambda b, j: (j, 0))
    consts = [wts[k] for k in ("w_pool", "pool_scale", "w_o_pool", "w_o_att", "ln2", "w_gate", "w_up",
                               "w_down", "ple_norm", "w_ple_gate", "w_ple_proj")]
    return pl.pallas_call(
        _post_kernel,
        grid=(B, nj),
        in_specs=[row(POOL_W), prev, nxt, inv, row(N_HEADS * V_DIM), row(D_MODEL), row(PLE_DIM)]
        + [_const_spec(a.shape) for a in consts],
        out_specs=row(D_MODEL),
        out_shape=jax.ShapeDtypeStruct((B, S, D_MODEL), F32),
        compiler_params=pltpu.CompilerParams(
            dimension_semantics=("arbitrary", "arbitrary"), vmem_limit_bytes=VMEM_LIMIT_BYTES),
        name="post",
    )(pool_in, pool_in, pool_in, inv_cnt, y_att, x, p, *consts)


def _pool_inv_counts(S):
    t = jnp.arange(S)
    cols = []
    for w in POOL_WINDOWS:
        cnt = jnp.clip(t - w // 2 + w, 0, S) - jnp.clip(t - w // 2, 0, S)
        cols.append(jnp.broadcast_to((1.0 / cnt.astype(F32))[:, None], (S, POOL_GC)))
    return jnp.concatenate(cols, axis=1)


def _rope_tables(S, q_norm, k_norm):
    inv = ROPE_BASE ** (-jnp.arange(0, QK_ROPE, 2, dtype=F32) / QK_ROPE)
    ang = jnp.arange(S, dtype=F32)[:, None] * inv[None, :]
    cos, sin = jnp.cos(ang), jnp.sin(ang)
    nope = lambda g: jnp.broadcast_to(g[None, :QK_NOPE], (S, QK_NOPE))
    g1 = lambda g: g[None, QK_NOPE:QK_NOPE + HALF_ROPE]
    g2 = lambda g: g[None, QK_NOPE + HALF_ROPE:]
    tq = jnp.concatenate([nope(q_norm), g1(q_norm) * cos, g2(q_norm) * cos, -g2(q_norm) * sin, g1(q_norm) * sin],
                         axis=1) * (ATTN_SCALE * math.log2(math.e))
    zeros = jnp.zeros((S, QK_NOPE), F32)
    ta = jnp.concatenate([zeros] + [g1(k_norm) * cos, g2(k_norm) * cos] * 2, axis=1)
    tb = jnp.concatenate([zeros] + [-g2(k_norm) * sin, g1(k_norm) * sin] * 2, axis=1)
    return tq, ta, tb


def _prepare_weights(ln1, w_in, w_pool, pool_scale, q_a_norm, w_qb, kv_a_norm, w_kvb, q_norm, k_norm, w_o,
                     ln2, w_gate, w_up, w_down, ple_norm, w_ple_gate, w_ple_proj):
    row = lambda a: a.reshape(1, -1).astype(F32)
    split = POOL_W + Q_LORA + KV_LORA
    k1, k2 = w_in[:, split:split + HALF_ROPE], w_in[:, split + HALF_ROPE:]
    w_in_pad = jnp.concatenate([w_in[:, :split], k2, k1, k2, k1, k1, k2, k1, k2], axis=1)
    qb = w_qb.reshape(Q_LORA, N_HEADS, QK_DIM)
    q1, q2 = qb[..., QK_NOPE:QK_NOPE + HALF_ROPE], qb[..., QK_NOPE + HALF_ROPE:]
    w_qb_pad = jnp.concatenate([qb[..., :QK_NOPE], q1, q2, q2, q1], axis=-1)
    kvb = w_kvb.reshape(KV_LORA, N_HEADS, QK_NOPE + V_DIM)
    w_kn = jnp.pad(kvb[..., :QK_NOPE], ((0, 0), (0, 0), (0, HEAD_SLOT - QK_NOPE)))
    wv = kvb[..., QK_NOPE:]
    even = (jnp.arange(N_HEADS) % 2 == 0)[None, :, None]
    zv = jnp.zeros_like(wv)
    w_v = jnp.where(even, jnp.concatenate([wv, zv], -1), jnp.concatenate([zv, wv], -1))
    lane = jnp.arange(HEAD_SLOT)[None, :]
    v_one = jnp.where(even[0], lane >= V_DIM, lane < V_DIM).astype(F32).reshape(1, QK_SLOTS)
    k_gain = jnp.pad(k_norm[:QK_NOPE], (0, HEAD_SLOT - QK_NOPE)).reshape(1, HEAD_SLOT).astype(F32)
    return {
        "ln1": row(ln1), "w_in": w_in_pad.astype(BF16), "q_a_norm": row(q_a_norm),
        "w_qb": w_qb_pad.reshape(Q_LORA, QK_SLOTS).astype(BF16), "kv_a_norm": row(kv_a_norm),
        "w_kn": w_kn.reshape(KV_LORA, QK_SLOTS).astype(BF16),
        "w_v": w_v.reshape(KV_LORA, QK_SLOTS).astype(BF16), "v_one": v_one, "k_gain": k_gain,
        "w_pool": w_pool.astype(BF16), "pool_scale": row(pool_scale),
        "w_o_pool": w_o[:POOL_W].astype(BF16), "w_o_att": w_o[POOL_W:].astype(BF16),
        "ln2": row(ln2), "w_gate": w_gate.astype(BF16), "w_up": w_up.astype(BF16),
        "w_down": w_down.astype(BF16), "ple_norm": row(ple_norm),
        "w_ple_gate": w_ple_gate.astype(BF16), "w_ple_proj": w_ple_proj.astype(BF16),
    }


def _layer(x, p, wts, tables, inv_cnt):
    pool_in, q, k, v = _projection(x, wts, tables, tm=512, sub=256)
    y_att = _attention(q, k, v, tq=512, heads=4)
    return _post(pool_in, y_att, x, p, wts, inv_cnt, tm=512)


def kernel(x_prompt, x_sample, p_prompt, p_sample, ln1, w_in, w_pool, pool_scale, q_a_norm, w_qb, kv_a_norm,
           w_kvb, q_norm, k_norm, w_o, ln2, w_gate, w_up, w_down, ple_norm, w_ple_gate, w_ple_proj):
    params = (ln1, w_in, w_pool, pool_scale, q_a_norm, w_qb, kv_a_norm, w_kvb, q_norm, k_norm, w_o, ln2,
              w_gate, w_up, w_down, ple_norm, w_ple_gate, w_ple_proj)
    wts = _prepare_weights(*[a[0] for a in params])
    tables = _rope_tables(x_prompt.shape[1], q_norm[0], k_norm[0])
    inv_cnt = _pool_inv_counts(x_prompt.shape[1])
    y_prompt = _layer(x_prompt, p_prompt[0], wts, tables, inv_cnt)
    y_sample = _layer(x_sample, p_sample[0], wts, tables, inv_cnt)
    return (y_prompt, y_sample)
```

```python
import functools
import math

import jax
import jax.numpy as jnp
from jax import lax
from jax.experimental import pallas as pl
from jax.experimental.pallas import tpu as pltpu

D_MODEL = 1024
POOL_W = 512
POOL_WINDOWS = (2, 4, 8, 16)
POOL_GC = 128
N_HEADS = 8
QK_NOPE = 64
QK_ROPE = 32
QK_DIM = QK_NOPE + QK_ROPE
V_DIM = 64
Q_LORA = 384
KV_LORA = 256
ROPE_BASE = 10000.0
ATTN_SCALE = 1.0 / math.sqrt(QK_DIM)
D_FF = 2816
PLE_DIM = 256
EPS = 1e-6

LANES = 128
SUBLANES = 8
HEAD_SLOT = LANES
HALF_ROPE = QK_ROPE // 2
QK_SLOTS = N_HEADS * HEAD_SLOT
IN_W_PAD = POOL_W + Q_LORA + KV_LORA + HEAD_SLOT
POOL_HALO = SUBLANES
VMEM_LIMIT_BYTES = 60 * 1024 * 1024

F32 = jnp.float32
BF16 = jnp.bfloat16


def _dot(a, b):
    return jnp.dot(a, b, preferred_element_type=F32)


def _rms(x, g, n):
    ms = jnp.sum(x * x, axis=-1, keepdims=True) * (1.0 / n)
    return x * lax.rsqrt(ms + EPS) * g


def _const_spec(shape):
    return pl.BlockSpec(shape, lambda *_: (0,) * len(shape), pipeline_mode=pl.Buffered(1))


def _proj_step(x_ref, ln1_ref, w_in_ref, qa_ref, w_qb_ref, kva_ref, w_kn_ref, w_v_ref, vone_ref,
               kg_ref, tq_ref, ta_ref, tb_ref, pool_ref, q_ref, k_ref, v_ref, lat_new_ref, lat_old_ref):
    tm = x_ref.shape[1]
    u = _rms(x_ref[0], ln1_ref[...], D_MODEL).astype(BF16)
    z = _dot(u, w_in_ref[...])
    pool_ref[0] = z[:, :POOL_W]
    lat_new_ref[...] = z[:, POOL_W:]

    lat = lat_old_ref[...]
    cq = _rms(lat[:, :Q_LORA], qa_ref[...], Q_LORA).astype(BF16)
    ckv = _rms(lat[:, Q_LORA:Q_LORA + KV_LORA], kva_ref[...], KV_LORA).astype(BF16)
    zr = lat[:, Q_LORA + KV_LORA:]
    k_rope = zr * ta_ref[...] + pltpu.roll(zr, HEAD_SLOT // 2, 1) * tb_ref[...]
    ss_rope = jnp.sum(zr * zr, axis=-1, keepdims=True) * 0.25
    qf = _dot(cq, w_qb_ref[...])
    kn = _dot(ckv, w_kn_ref[...])
    vv = _dot(ckv, w_v_ref[...])
    v_ref[0] = (vv + vone_ref[...]).astype(BF16)
    lane = lax.broadcasted_iota(jnp.int32, (tm, HEAD_SLOT), 1)
    tq = tq_ref[...]
    for h in range(N_HEADS):
        sl = slice(h * HEAD_SLOT, (h + 1) * HEAD_SLOT)
        xq = qf[:, sl]
        xq96 = jnp.where(lane < QK_DIM, xq, 0.0)
        rq = lax.rsqrt(jnp.sum(xq96 * xq96, axis=-1, keepdims=True) * (1.0 / QK_DIM) + EPS)
        q_ref[0, :, sl] = (xq * rq * tq).astype(BF16)
        xk = kn[:, sl]
        rk = lax.rsqrt((jnp.sum(xk * xk, axis=-1, keepdims=True) + ss_rope) * (1.0 / QK_DIM) + EPS)
        k_ref[0, :, sl] = ((xk * kg_ref[...] + k_rope) * rk).astype(BF16)


def _proj_kernel(*refs):
    io_refs, (lat_even_ref, lat_odd_ref) = refs[:-2], refs[-2:]
    t = pl.program_id(0)

    @pl.when(t == 0)
    def _():
        lat_odd_ref[...] = jnp.zeros_like(lat_odd_ref)

    @pl.when(t % 2 == 0)
    def _():
        _proj_step(*io_refs, lat_even_ref, lat_odd_ref)

    @pl.when(t % 2 == 1)
    def _():
        _proj_step(*io_refs, lat_odd_ref, lat_even_ref)


def _projection(x, wts, tables, tm):
    B, S, _ = x.shape
    nj = S // tm
    n_tiles = B * nj

    def cur(t):
        t = jnp.minimum(t, n_tiles - 1)
        return t // nj, t % nj, 0

    def lag(t):
        t = jnp.maximum(t - 1, 0)
        return t // nj, t % nj, 0

    lag_row = lambda w: pl.BlockSpec((1, tm, w), lag)
    tab = pl.BlockSpec((tm, HEAD_SLOT), lambda t: (jnp.maximum(t - 1, 0) % nj, 0))
    consts = [wts[k] for k in ("ln1", "w_in", "q_a_norm", "w_qb", "kv_a_norm", "w_kn", "w_v", "v_one",
                               "k_gain")]
    lat_w = IN_W_PAD - POOL_W
    return pl.pallas_call(
        _proj_kernel,
        grid=(n_tiles + 1,),
        in_specs=[pl.BlockSpec((1, tm, D_MODEL), cur)] + [_const_spec(a.shape) for a in consts] + [tab, tab, tab],
        out_specs=[pl.BlockSpec((1, tm, POOL_W), cur), lag_row(QK_SLOTS), lag_row(QK_SLOTS), lag_row(QK_SLOTS)],
        out_shape=[jax.ShapeDtypeStruct((B, S, POOL_W), F32)]
        + [jax.ShapeDtypeStruct((B, S, QK_SLOTS), BF16)] * 3,
        scratch_shapes=[pltpu.VMEM((tm, lat_w), F32), pltpu.VMEM((tm, lat_w), F32)],
        compiler_params=pltpu.CompilerParams(
            dimension_semantics=("arbitrary",), vmem_limit_bytes=VMEM_LIMIT_BYTES),
        name="projection",
    )(x, *consts, *tables)


def _attn_step(q_ref, k_ref, v_ref, o_ref, new_refs, old_refs):
    s_new_ref, m_new_ref = new_refs
    s_old_ref, m_old_ref = old_refs
    tq = q_ref.shape[1]
    heads = s_new_ref.shape[0]
    for hh in range(heads):
        sl = slice(hh * HEAD_SLOT, (hh + 1) * HEAD_SLOT)
        s = lax.dot_general(q_ref[0, :, sl], k_ref[0, :, sl], (((1,), (1,)), ((), ())),
                            preferred_element_type=F32)
        s_new_ref[hh] = s
        m_new_ref[hh] = jnp.max(s, axis=-1, keepdims=True)
    lane = lax.broadcasted_iota(jnp.int32, (tq, HEAD_SLOT), 1)
    outs = []
    for hh in range(heads):
        sl = slice(hh * HEAD_SLOT, (hh + 1) * HEAD_SLOT)
        p = jnp.exp2(s_old_ref[hh] - m_old_ref[hh]).astype(BF16)
        o = _dot(p, v_ref[0, :, sl])
        outs.append(o / pltpu.roll(o, HEAD_SLOT // 2, 1))
    for pr in range(heads // 2):
        o_ref[0, :, pr * HEAD_SLOT:(pr + 1) * HEAD_SLOT] = jnp.where(
            lane < V_DIM, outs[2 * pr], outs[2 * pr + 1]).astype(o_ref.dtype)


def _attn_kernel(q_ref, k_ref, v_ref, o_ref, s_even_ref, m_even_ref, s_odd_ref, m_odd_ref):
    t = pl.program_id(0)
    even, odd = (s_even_ref, m_even_ref), (s_odd_ref, m_odd_ref)

    @pl.when(t == 0)
    def _():
        s_odd_ref[...] = jnp.zeros_like(s_odd_ref)
        m_odd_ref[...] = jnp.zeros_like(m_odd_ref)

    @pl.when(t % 2 == 0)
    def _():
        _attn_step(q_ref, k_ref, v_ref, o_ref, even, odd)

    @pl.when(t % 2 == 1)
    def _():
        _attn_step(q_ref, k_ref, v_ref, o_ref, odd, even)


def _attention(q, k, v, tq, heads):
    B, S, _ = q.shape
    width = heads * HEAD_SLOT
    n_grp = N_HEADS // heads
    nq = S // tq
    n_blocks = B * n_grp * nq

    def unravel(t):
        return t // (n_grp * nq), (t // nq) % n_grp, t % nq

    def q_map(t):
        b, h, i = unravel(jnp.minimum(t, n_blocks - 1))
        return b, i, h

    def k_map(t):
        b, h, _ = unravel(jnp.minimum(t, n_blocks - 1))
        return b, 0, h

    def v_map(t):
        b, h, _ = unravel(jnp.maximum(t - 1, 0))
        return b, 0, h

    def o_map(t):
        b, h, i = unravel(jnp.maximum(t - 1, 0))
        return b, i, h

    return pl.pallas_call(
        _attn_kernel,
        grid=(n_blocks + 1,),
        in_specs=[pl.BlockSpec((1, tq, width), q_map),
                  pl.BlockSpec((1, S, width), k_map),
                  pl.BlockSpec((1, S, width), v_map)],
        out_specs=pl.BlockSpec((1, tq, heads * V_DIM), o_map),
        out_shape=jax.ShapeDtypeStruct((B, S, N_HEADS * V_DIM), BF16),
        scratch_shapes=[pltpu.VMEM((heads, tq, S), F32), pltpu.VMEM((heads, tq, 1), F32)] * 2,
        compiler_params=pltpu.CompilerParams(
            dimension_semantics=("arbitrary",), vmem_limit_bytes=VMEM_LIMIT_BYTES),
        name="attention",
    )(q, k, v)


def _post_kernel(pool_ref, prev_ref, next_ref, inv_ref, yatt_ref, x_ref, p_ref, w_pool_ref, pscale_ref, w_op_ref,
                 w_oa_ref, ln2_ref, w_gate_ref, w_up_ref, w_down_ref, plen_ref, w_pg_ref, w_pp_ref, o_ref):
    tm = pool_ref.shape[1]
    j = pl.program_id(1)
    ext = jnp.concatenate([jnp.where(j > 0, prev_ref[0], 0.0), pool_ref[0],
                           jnp.where(j < pl.num_programs(1) - 1, next_ref[0], 0.0)], axis=0)
    n_ext = tm + 2 * POOL_HALO
    ahead_by = lambda a, s: pltpu.roll(a, n_ext - s, 0)
    ys = []
    for g, w in enumerate(POOL_WINDOWS):
        sl = slice(g * POOL_GC, (g + 1) * POOL_GC)
        acc = ext[:, sl]
        span = 1
        while span < w:
            acc = acc + ahead_by(acc, span)
            span *= 2
        acc = ahead_by(acc, POOL_HALO - w // 2)[0:tm] if w // 2 < POOL_HALO else acc[0:tm]
        diff = acc * inv_ref[:, sl] - pool_ref[0, :, sl]
        ys.append(_dot(diff.astype(BF16), w_pool_ref[g]))
    y_pool = (jnp.concatenate(ys, axis=1) * pscale_ref[...]).astype(BF16)

    h = x_ref[0] + _dot(y_pool, w_op_ref[...]) + _dot(yatt_ref[0], w_oa_ref[...])

    u2 = _rms(h, ln2_ref[...], D_MODEL).astype(BF16)
    act = (jax.nn.silu(_dot(u2, w_gate_ref[...])) * _dot(u2, w_up_ref[...])).astype(BF16)
    h = h + _dot(act, w_down_ref[...])

    gate = jax.nn.sigmoid(_dot(_rms(h, plen_ref[...], D_MODEL).astype(BF16), w_pg_ref[...]))
    o_ref[0] = h + gate * _dot(p_ref[0].astype(BF16), w_pp_ref[...])


def _post(pool_in, y_att, x, p, wts, inv_cnt, tm):
    B, S, _ = x.shape
    nj = S // tm
    hb = tm // POOL_HALO
    row = lambda w: pl.BlockSpec((1, tm, w), lambda b, j: (b, j, 0))
    prev = pl.BlockSpec((1, POOL_HALO, POOL_W), lambda b, j: (b, jnp.maximum(j * hb - 1, 0), 0))
    nxt = pl.BlockSpec((1, POOL_HALO, POOL_W),
                       lambda b, j: (b, jnp.minimum((j + 1) * hb, S // POOL_HALO - 1), 0))
    inv = pl.BlockSpec((tm, POOL_W), lambda b, j: (j, 0))
    consts = [wts[k] for k in ("w_pool", "pool_scale", "w_o_pool", "w_o_att", "ln2", "w_gate", "w_up",
                               "w_down", "ple_norm", "w_ple_gate", "w_ple_proj")]
    return pl.pallas_call(
        _post_kernel,
        grid=(B, nj),
        in_specs=[row(POOL_W), prev, nxt, inv, row(N_HEADS * V_DIM), row(D_MODEL), row(PLE_DIM)]
        + [_const_spec(a.shape) for a in consts],
        out_specs=row(D_MODEL),
        out_shape=jax.ShapeDtypeStruct((B, S, D_MODEL), F32),
        compiler_params=pltpu.CompilerParams(
            dimension_semantics=("arbitrary", "arbitrary"), vmem_limit_bytes=VMEM_LIMIT_BYTES),
        name="post",
    )(pool_in, pool_in, pool_in, inv_cnt, y_att, x, p, *consts)


def _pool_inv_counts(S):
    t = jnp.arange(S)
    cols = []
    for w in POOL_WINDOWS:
        cnt = jnp.clip(t - w // 2 + w, 0, S) - jnp.clip(t - w // 2, 0, S)
        cols.append(jnp.broadcast_to((1.0 / cnt.astype(F32))[:, None], (S, POOL_GC)))
    return jnp.concatenate(cols, axis=1)


def _rope_tables(S, q_norm, k_norm):
    inv = ROPE_BASE ** (-jnp.arange(0, QK_ROPE, 2, dtype=F32) / QK_ROPE)
    ang = jnp.arange(S, dtype=F32)[:, None] * inv[None, :]
    cos, sin = jnp.cos(ang), jnp.sin(ang)
    nope = lambda g: jnp.broadcast_to(g[None, :QK_NOPE], (S, QK_NOPE))
    g1 = lambda g: g[None, QK_NOPE:QK_NOPE + HALF_ROPE]
    g2 = lambda g: g[None, QK_NOPE + HALF_ROPE:]
    tq = jnp.concatenate([nope(q_norm), g1(q_norm) * cos, g2(q_norm) * cos, -g2(q_norm) * sin, g1(q_norm) * sin],
                         axis=1) * (ATTN_SCALE * math.log2(math.e))
    zeros = jnp.zeros((S, QK_NOPE), F32)
    ta = jnp.concatenate([zeros] + [g1(k_norm) * cos, g2(k_norm) * cos] * 2, axis=1)
    tb = jnp.concatenate([zeros] + [-g2(k_norm) * sin, g1(k_norm) * sin] * 2, axis=1)
    return tq, ta, tb


def _prepare_weights(ln1, w_in, w_pool, pool_scale, q_a_norm, w_qb, kv_a_norm, w_kvb, q_norm, k_norm, w_o,
                     ln2, w_gate, w_up, w_down, ple_norm, w_ple_gate, w_ple_proj):
    row = lambda a: a.reshape(1, -1).astype(F32)
    split = POOL_W + Q_LORA + KV_LORA
    k1, k2 = w_in[:, split:split + HALF_ROPE], w_in[:, split + HALF_ROPE:]
    w_in_pad = jnp.concatenate([w_in[:, :split], k2, k1, k2, k1, k1, k2, k1, k2], axis=1)
    qb = w_qb.reshape(Q_LORA, N_HEADS, QK_DIM)
    q1, q2 = qb[..., QK_NOPE:QK_NOPE + HALF_ROPE], qb[..., QK_NOPE + HALF_ROPE:]
    w_qb_pad = jnp.concatenate([qb[..., :QK_NOPE], q1, q2, q2, q1], axis=-1)
    kvb = w_kvb.reshape(KV_LORA, N_HEADS, QK_NOPE + V_DIM)
    w_kn = jnp.pad(kvb[..., :QK_NOPE], ((0, 0), (0, 0), (0, HEAD_SLOT - QK_NOPE)))
    wv = kvb[..., QK_NOPE:]
    even = (jnp.arange(N_HEADS) % 2 == 0)[None, :, None]
    zv = jnp.zeros_like(wv)
    w_v = jnp.where(even, jnp.concatenate([wv, zv], -1), jnp.concatenate([zv, wv], -1))
    lane = jnp.arange(HEAD_SLOT)[None, :]
    v_one = jnp.where(even[0], lane >= V_DIM, lane < V_DIM).astype(F32).reshape(1, QK_SLOTS)
    k_gain = jnp.pad(k_norm[:QK_NOPE], (0, HEAD_SLOT - QK_NOPE)).reshape(1, HEAD_SLOT).astype(F32)
    return {
        "ln1": row(ln1), "w_in": w_in_pad.astype(BF16), "q_a_norm": row(q_a_norm),
        "w_qb": w_qb_pad.reshape(Q_LORA, QK_SLOTS).astype(BF16), "kv_a_norm": row(kv_a_norm),
        "w_kn": w_kn.reshape(KV_LORA, QK_SLOTS).astype(BF16),
        "w_v": w_v.reshape(KV_LORA, QK_SLOTS).astype(BF16), "v_one": v_one, "k_gain": k_gain,
        "w_pool": w_pool.astype(BF16), "pool_scale": row(pool_scale),
        "w_o_pool": w_o[:POOL_W].astype(BF16), "w_o_att": w_o[POOL_W:].astype(BF16),
        "ln2": row(ln2), "w_gate": w_gate.astype(BF16), "w_up": w_up.astype(BF16),
        "w_down": w_down.astype(BF16), "ple_norm": row(ple_norm),
        "w_ple_gate": w_ple_gate.astype(BF16), "w_ple_proj": w_ple_proj.astype(BF16),
    }


def _layer(x, p, wts, tables, inv_cnt):
    pool_in, q, k, v = _projection(x, wts, tables, tm=512)
    y_att = _attention(q, k, v, tq=512, heads=4)
    return _post(pool_in, y_att, x, p, wts, inv_cnt, tm=512)


def kernel(x_prompt, x_sample, p_prompt, p_sample, ln1, w_in, w_pool, pool_scale, q_a_norm, w_qb, kv_a_norm,
           w_kvb, q_norm, k_norm, w_o, ln2, w_gate, w_up, w_down, ple_norm, w_ple_gate, w_ple_proj):
    params = (ln1, w_in, w_pool, pool_scale, q_a_norm, w_qb, kv_a_norm, w_kvb, q_norm, k_norm, w_o, ln2,
              w_gate, w_up, w_down, ple_norm, w_ple_gate, w_ple_proj)
    wts = _prepare_weights(*[a[0] for a in params])
    tables = _rope_tables(x_prompt.shape[1], q_norm[0], k_norm[0])
    inv_cnt = _pool_inv_counts(x_prompt.shape[1])
    y_prompt = _layer(x_prompt, p_prompt[0], wts, tables, inv_cnt)
    y_sample = _layer(x_sample, p_sample[0], wts, tables, inv_cnt)
    return (y_prompt, y_sample)
```
